```python
import math
import jax, jax.numpy as jnp
from jax import lax
import numpy as np

D_MODEL = 2048
BATCH = 8
SEQ = 2048
DEPTH = 1
DEC_BATCH = 32
DEC_SEQ = 4
PAST_LEN = 8192
PAGE_SIZE = 128

N_META = 16
N_HEADS = 8
HEAD_DIM = 64
VAL_DIM = 2 * HEAD_DIM
ATTN_DIM = N_HEADS * VAL_DIM
CONV_DIM = 1024
CONV_WIDTH = 31
D_FF = -(-8 * D_MODEL // (3 * 256)) * 256
IN_COLS = 3 * ATTN_DIM + 2 * CONV_DIM + 2 * D_MODEL
ROPE_THETA = 10000.0
Q_BLOCK = 128
EPS = 1e-6
NEG = -1e30

kernel_name = 'gated_conformer_diffattn_decoder_step'


def rms_norm(x, g):
    xf = x.astype(jnp.float32)
    y = xf * lax.rsqrt(jnp.mean(xf * xf, axis=-1, keepdims=True) + EPS)
    return (y * g.astype(jnp.float32)).astype(x.dtype)


def layer_norm(x, g, b):
    xf = x.astype(jnp.float32)
    mu = jnp.mean(xf, axis=-1, keepdims=True)
    xc = xf - mu
    y = xc * lax.rsqrt(jnp.mean(xc * xc, axis=-1, keepdims=True) + EPS)
    return (y * g.astype(jnp.float32) + b.astype(jnp.float32)).astype(x.dtype)


def rope(x, pos):
    half = HEAD_DIM // 2
    inv = ROPE_THETA ** (-jnp.arange(half, dtype=jnp.float32) / half)
    ang = pos.astype(jnp.float32)[:, None] * inv[None, :]
    cos = jnp.cos(ang)[:, None, None, :]
    sin = jnp.sin(ang)[:, None, None, :]
    xf = x.astype(jnp.float32)
    x1, x2 = xf[..., :half], xf[..., half:]
    return jnp.concatenate([x1 * cos - x2 * sin, x2 * cos + x1 * sin], axis=-1).astype(x.dtype)


def diff_attend(q, k, v, qpos, kpos, lam):
    s = jnp.einsum('bqhcd,bkhcd->bchqk', q, k).astype(jnp.float32) * (HEAD_DIM ** -0.5)
    mask = kpos[None, :] <= qpos[:, None]
    s = jnp.where(mask, s, NEG)
    a = jax.nn.softmax(s, axis=-1)
    w = a[:, 0] - lam * a[:, 1]
    return jnp.einsum('bhqk,bkhe->bqhe', w.astype(v.dtype), v)


def prompt_attention(q, k, v, lam):
    B, L = q.shape[0], q.shape[1]
    Lp = -(-L // Q_BLOCK) * Q_BLOCK
    nblk = Lp // Q_BLOCK
    qp = jnp.pad(q, ((0, 0), (0, Lp - L), (0, 0), (0, 0), (0, 0)))
    qb = qp.reshape(B, nblk, Q_BLOCK, N_HEADS, 2, HEAD_DIM).transpose(1, 0, 2, 3, 4, 5)
    qposb = jnp.arange(Lp, dtype=jnp.int32).reshape(nblk, Q_BLOCK)
    kpos = jnp.arange(L, dtype=jnp.int32)
    out = lax.map(lambda a: diff_attend(a[0], k, v, a[1], kpos, lam), (qb, qposb))
    out = out.transpose(1, 0, 2, 3, 4).reshape(B, Lp, N_HEADS, VAL_DIM)
    return out[:, :L]


def causal_depthwise(u, buf, w, b):
    up = jnp.concatenate([buf, u], axis=1)
    y = lax.conv_general_dilated(up, w[:, None, :], window_strides=(1,), padding='VALID',
                                 dimension_numbers=('NWC', 'WIO', 'NWC'),
                                 feature_group_count=CONV_DIM)
    return y + b, up[:, -(CONV_WIDTH - 1):]


def hybrid_layer(x, pos, conv_buf, attend, lw, layer_idx):
    B, T, _ = x.shape
    xn = rms_norm(x, lw['norm_mix_g'])
    proj = xn @ lw['w_in']
    q, k, v, glu, g_att, g_conv = jnp.split(
        proj, [ATTN_DIM, 2 * ATTN_DIM, 3 * ATTN_DIM, 3 * ATTN_DIM + 2 * CONV_DIM,
               3 * ATTN_DIM + 2 * CONV_DIM + D_MODEL], axis=-1)
    q = rope(q.reshape(B, T, N_HEADS, 2, HEAD_DIM), pos)
    k = rope(k.reshape(B, T, N_HEADS, 2, HEAD_DIM), pos)
    v = v.reshape(B, T, N_HEADS, VAL_DIM)
    lam_init = 0.8 - 0.6 * math.exp(-0.3 * layer_idx)
    lam = (jnp.exp(jnp.sum(lw['lambda_q1'].astype(jnp.float32) * lw['lambda_k1'].astype(jnp.float32)))
           - jnp.exp(jnp.sum(lw['lambda_q2'].astype(jnp.float32) * lw['lambda_k2'].astype(jnp.float32)))
           + lam_init)
    o = attend(q, k, v, lam)
    o = rms_norm(o, lw['subln_g']) * (1.0 - lam_init)
    y_att = o.reshape(B, T, ATTN_DIM) @ lw['w_attn_proj']
    a, gt = jnp.split(glu, 2, axis=-1)
    u = a * jax.nn.sigmoid(gt)
    c, new_buf = causal_depthwise(u, conv_buf, lw['conv_dw_w'], lw['conv_dw_b'])
    c = jax.nn.silu(layer_norm(c, lw['conv_ln_g'], lw['conv_ln_b']))
    y_conv = c @ lw['w_conv_proj']
    h = jax.nn.sigmoid(g_att) * y_att + jax.nn.sigmoid(g_conv) * y_conv
    x = x + h @ lw['w_out']
    xn = rms_norm(x, lw['norm_ffn_g'])
    x = x + (jax.nn.silu(xn @ lw['w_ffn_gate']) * (xn @ lw['w_ffn_up'])) @ lw['w_ffn_down']
    return (x, k.reshape(B, T, N_HEADS, 2 * HEAD_DIM), v, new_buf)


def setup_inputs(seed: int = 0) -> dict:
    key = jax.random.key(seed)
    ks = jax.random.split(key, 32)
    f32 = jnp.float32
    n_pages = PAST_LEN // PAGE_SIZE
    n_pool = (DEC_BATCH * n_pages * 5) // 4

    def nrm(k, shape, scale):
        return jax.random.normal(k, shape, f32) * scale

    def gain(k, shape):
        return 1.0 + 0.01 * jax.random.normal(k, shape, f32)

    perm = jax.random.permutation(ks[3], n_pool)[:DEC_BATCH * n_pages]
    page_table = perm.reshape(DEC_BATCH, n_pages).astype(jnp.int32)
    return {
        'x_prompt': nrm(ks[0], (BATCH, SEQ, D_MODEL), 1.0),
        'x_sample': nrm(ks[1], (DEC_BATCH, DEC_SEQ, D_MODEL), 1.0),
        'cache_k': nrm(ks[2], (DEPTH, n_pool, PAGE_SIZE, N_HEADS, 2 * HEAD_DIM), 1.0),
        'cache_v': nrm(ks[4], (DEPTH, n_pool, PAGE_SIZE, N_HEADS, VAL_DIM), 1.0),
        'state_conv': nrm(ks[5], (DEPTH, DEC_BATCH, CONV_WIDTH - 1, CONV_DIM), 1.0),
        'page_table': page_table,
        'meta_tokens': nrm(ks[6], (N_META, D_MODEL), 1.0),
        'norm_mix_g': gain(ks[7], (DEPTH, D_MODEL)),
        'w_in': nrm(ks[8], (DEPTH, D_MODEL, IN_COLS), D_MODEL ** -0.5),
        'lambda_q1': nrm(ks[9], (DEPTH, HEAD_DIM), 0.1),
        'lambda_k1': nrm(ks[10], (DEPTH, HEAD_DIM), 0.1),
        'lambda_q2': nrm(ks[11], (DEPTH, HEAD_DIM), 0.1),
        'lambda_k2': nrm(ks[12], (DEPTH, HEAD_DIM), 0.1),
        'subln_g': gain(ks[13], (DEPTH, VAL_DIM)),
        'w_attn_proj': nrm(ks[14], (DEPTH, ATTN_DIM, D_MODEL), ATTN_DIM ** -0.5),
        'conv_dw_w': nrm(ks[15], (DEPTH, CONV_WIDTH, CONV_DIM), CONV_WIDTH ** -0.5),
        'conv_dw_b': nrm(ks[16], (DEPTH, CONV_DIM), 0.01),
        'conv_ln_g': gain(ks[17], (DEPTH, CONV_DIM)),
        'conv_ln_b': nrm(ks[18], (DEPTH, CONV_DIM), 0.01),
        'w_conv_proj': nrm(ks[19], (DEPTH, CONV_DIM, D_MODEL), CONV_DIM ** -0.5),
        'w_out': nrm(ks[20], (DEPTH, D_MODEL, D_MODEL), D_MODEL ** -0.5),
        'norm_ffn_g': gain(ks[21], (DEPTH, D_MODEL)),
        'w_ffn_gate': nrm(ks[22], (DEPTH, D_MODEL, D_FF), D_MODEL ** -0.5),
        'w_ffn_up': nrm(ks[23], (DEPTH, D_MODEL, D_FF), D_MODEL ** -0.5),
        'w_ffn_down': nrm(ks[24], (DEPTH, D_FF, D_MODEL), D_FF ** -0.5),
        'norm_final_g': gain(ks[25], (D_MODEL,)),
    }


def reference(x_prompt, x_sample, cache_k, cache_v, state_conv, page_table, meta_tokens,
              norm_mix_g, w_in, lambda_q1, lambda_k1, lambda_q2, lambda_k2, subln_g,
              w_attn_proj, conv_dw_w, conv_dw_b, conv_ln_g, conv_ln_b, w_conv_proj, w_out,
              norm_ffn_g, w_ffn_gate, w_ffn_up, w_ffn_down, norm_final_g):
    B = x_prompt.shape[0]
    DB, T = x_sample.shape[0], x_sample.shape[1]
    n_pages = page_table.shape[1]
    past_len = n_pages * PAGE_SIZE

    meta = jnp.broadcast_to(meta_tokens.astype(x_prompt.dtype)[None], (B, N_META, D_MODEL))
    xp = jnp.concatenate([meta, x_prompt], axis=1)
    L = xp.shape[1]
    pos_p = jnp.arange(L, dtype=jnp.int32)
    pos_s = past_len + jnp.arange(T, dtype=jnp.int32)
    kpos_s = jnp.arange(past_len + T, dtype=jnp.int32)
    xs = x_sample

    kp_l, vp_l, cp_l, ks_l, vs_l, cs_l = [], [], [], [], [], []
    for layer in range(DEPTH):
        lw = {
            'norm_mix_g': norm_mix_g[layer], 'w_in': w_in[layer],
            'lambda_q1': lambda_q1[layer], 'lambda_k1': lambda_k1[layer],
            'lambda_q2': lambda_q2[layer], 'lambda_k2': lambda_k2[layer],
            'subln_g': subln_g[layer], 'w_attn_proj': w_attn_proj[layer],
            'conv_dw_w': conv_dw_w[layer], 'conv_dw_b': conv_dw_b[layer],
            'conv_ln_g': conv_ln_g[layer], 'conv_ln_b': conv_ln_b[layer],
            'w_conv_proj': w_conv_proj[layer], 'w_out': w_out[layer],
            'norm_ffn_g': norm_ffn_g[layer], 'w_ffn_gate': w_ffn_gate[layer],
            'w_ffn_up': w_ffn_up[layer], 'w_ffn_down': w_ffn_down[layer],
        }
        buf0 = jnp.zeros((B, CONV_WIDTH - 1, CONV_DIM), xp.dtype)
        xp, kp, vp, cp = hybrid_layer(xp, pos_p, buf0, prompt_attention, lw, layer)

        past_k = cache_k[layer][page_table].reshape(DB, past_len, N_HEADS, 2, HEAD_DIM)
        past_v = cache_v[layer][page_table].reshape(DB, past_len, N_HEADS, VAL_DIM)

        def sample_attention(q, k, v, lam, past_k=past_k, past_v=past_v):
            k_all = jnp.concatenate([past_k.astype(k.dtype), k], axis=1)
            v_all = jnp.concatenate([past_v.astype(v.dtype), v], axis=1)
            return diff_attend(q, k_all, v_all, pos_s, kpos_s, lam)

        xs, ks_, vs_, cs_ = hybrid_layer(xs, pos_s, state_conv[layer].astype(xs.dtype),
                                         sample_attention, lw, layer)
        kp_l.append(kp); vp_l.append(vp); cp_l.append(cp)
        ks_l.append(ks_); vs_l.append(vs_); cs_l.append(cs_)

    y_prompt = rms_norm(xp, norm_final_g)[:, N_META:]
    y_sample = rms_norm(xs, norm_final_g)
    k_prompt = jnp.stack(kp_l); v_prompt = jnp.stack(vp_l); conv_prompt = jnp.stack(cp_l)
    k_sample = jnp.stack(ks_l); v_sample = jnp.stack(vs_l); conv_sample = jnp.stack(cs_l)
    return (y_prompt, y_sample, k_prompt, v_prompt, conv_prompt, k_sample, v_sample, conv_sample)
```

```python
import functools
import math

import jax
import jax.numpy as jnp
from jax import lax
from jax.experimental import pallas as pl
from jax.experimental.pallas import tpu as pltpu

F32 = jnp.float32
BF16 = jnp.bfloat16

N_META = 16
N_HEADS = 8
HEAD_DIM = 64
VAL_DIM = 2 * HEAD_DIM
ATTN_DIM = N_HEADS * VAL_DIM
CONV_DIM = 1024
CONV_WIDTH = 31
PAGE_SIZE = 128
ROPE_THETA = 10000.0
EPS = 1e-6
NEG = -1e30
LAM_INIT = 0.8 - 0.6 * math.exp(-0.3 * 0)

LANES = 128
HIST_ROWS = 32
HIST_PAD = HIST_ROWS - (CONV_WIDTH - 1)
VMEM_LIMIT = 56 * 1024 * 1024
COL_TILE = 1024
PAGES_PER_STEP = 8


def _dot(a, b):
    return jnp.dot(a, b, preferred_element_type=F32)


def _dot_nt(a, b):
    return lax.dot_general(a, b, (((1,), (1,)), ((), ())), preferred_element_type=F32)


def _sigmoid(x):
    return 1.0 / (1.0 + jnp.exp(-x))


def _params(*sem):
    return pltpu.CompilerParams(dimension_semantics=sem, vmem_limit_bytes=VMEM_LIMIT)


def _inproj_kernel(x_ref, g_ref, w_ref, cos_ref, sin_ref,
                   q_ref, k_ref, kb_ref, v_ref, vb_ref, u_ref, ga_ref, gc_ref,
                   xn_ref, a_ref):
    j = pl.program_id(1)
    tm = x_ref.shape[0]

    @pl.when(j == 0)
    def _():
        x = x_ref[...]
        ms = jnp.mean(x * x, axis=-1, keepdims=True)
        xn_ref[...] = (x * lax.rsqrt(ms + EPS) * g_ref[...]).astype(BF16)

    acc = _dot(xn_ref[...], w_ref[...])

    def rope_heads():
        lane = lax.broadcasted_iota(jnp.int32, (tm, LANES), 1)
        first = (lane % HEAD_DIM) < (HEAD_DIM // 2)
        cos = cos_ref[...]
        sin = sin_ref[...]
        for h in range(N_HEADS):
            yh = acc[:, h * LANES:(h + 1) * LANES]
            rot = jnp.where(first, pltpu.roll(yh, LANES - HEAD_DIM // 2, 1),
                            pltpu.roll(yh, HEAD_DIM // 2, 1))
            yield h, yh * cos + rot * sin

    @pl.when(j == 0)
    def _():
        for h, r in rope_heads():
            q_ref[:, h * LANES:(h + 1) * LANES] = (r * (HEAD_DIM ** -0.5)).astype(BF16)

    @pl.when(j == 1)
    def _():
        for h, r in rope_heads():
            k_ref[:, h * LANES:(h + 1) * LANES] = r
            kb_ref[:, h * LANES:(h + 1) * LANES] = r.astype(BF16)

    @pl.when(j == 2)
    def _():
        v_ref[...] = acc
        vb_ref[...] = acc.astype(BF16)

    @pl.when(j == 3)
    def _():
        a_ref[...] = acc

    @pl.when(j == 4)
    def _():
        u_ref[...] = a_ref[...] * _sigmoid(acc)

    @pl.when((j == 5) | (j == 6))
    def _():
        ga_ref[...] = _sigmoid(acc).astype(BF16)

    @pl.when(j >= 7)
    def _():
        gc_ref[...] = _sigmoid(acc).astype(BF16)


def _inproj(x, g, w, cos_t, sin_t, tm):
    m, d = x.shape
    n_pos = cos_t.shape[0] // tm
    nj = w.shape[1] // COL_TILE
    row = lambda i, j: (i, 0)
    outs = (
        jax.ShapeDtypeStruct((m, ATTN_DIM), BF16),
        jax.ShapeDtypeStruct((m, ATTN_DIM), F32),
        jax.ShapeDtypeStruct((m, ATTN_DIM), BF16),
        jax.ShapeDtypeStruct((m, ATTN_DIM), F32),
        jax.ShapeDtypeStruct((m, ATTN_DIM), BF16),
        jax.ShapeDtypeStruct((m, CONV_DIM), F32),
        jax.ShapeDtypeStruct((m, d), BF16),
        jax.ShapeDtypeStruct((m, d), BF16),
    )
    blk = lambda: pl.BlockSpec((tm, COL_TILE), row)
    return pl.pallas_call(
        _inproj_kernel,
        grid=(m // tm, nj),
        in_specs=[
            pl.BlockSpec((tm, d), row),
            pl.BlockSpec((1, d), lambda i, j: (0, 0)),
            pl.BlockSpec((d, COL_TILE), lambda i, j: (0, j)),
            pl.BlockSpec((tm, LANES), lambda i, j: (i % n_pos, 0)),
            pl.BlockSpec((tm, LANES), lambda i, j: (i % n_pos, 0)),
        ],
        out_specs=(
            blk(), blk(), blk(), blk(), blk(), blk(),
            pl.BlockSpec((tm, COL_TILE), lambda i, j: (i, jnp.clip(j - 5, 0, 1))),
            pl.BlockSpec((tm, COL_TILE), lambda i, j: (i, jnp.clip(j - 7, 0, 1))),
        ),
        out_shape=outs,
        scratch_shapes=[pltpu.VMEM((tm, d), BF16), pltpu.VMEM((tm, COL_TILE), F32)],
        compiler_params=_params("parallel", "arbitrary"),
        name="inproj",
    )(x, g, w, cos_t, sin_t)


def _lambda(l4):
    a = jnp.sum(l4[0:1] * l4[1:2], axis=1, keepdims=True)
    b = jnp.sum(l4[2:3] * l4[3:4], axis=1, keepdims=True)
    return jnp.exp(a) - jnp.exp(b) + LAM_INIT


def _sub_ln(od, g):
    ms = jnp.mean(od * od, axis=-1, keepdims=True)
    return od * lax.rsqrt(ms + EPS) * g * (1.0 - LAM_INIT)


def _softmax_update(s, v, m, l, acc):
    m_new = jnp.maximum(m, jnp.max(s, axis=1, keepdims=True))
    alpha = jnp.exp(m - m_new)
    p = jnp.exp(s - m_new)
    l = alpha * l + jnp.sum(p, axis=1, keepdims=True)
    acc = alpha * acc + _dot(p.astype(BF16), v)
    return m_new, l, acc


def _pattn_kernel(l4_ref, q_ref, k_ref, v_ref, km_ref, vm_ref, g_ref, o_ref, *, tq):
    nq = q_ref.shape[0] // tq
    lam = _lambda(l4_ref[...])
    g = g_ref[...]
    first = lax.broadcasted_iota(jnp.int32, (tq, LANES), 1) < HEAD_DIM
    tri = (lax.broadcasted_iota(jnp.int32, (2 * tq, tq), 1)
           <= lax.broadcasted_iota(jnp.int32, (2 * tq, tq), 0) % tq)
    km = km_ref[...]
    vm = vm_ref[...]

    def q_tile(qi, carry):
        q0 = pl.multiple_of(qi * tq, tq)
        q = q_ref[pl.ds(q0, tq), :]
        zero = jnp.zeros_like(q)
        qq = jnp.concatenate([jnp.where(first, q, zero), jnp.where(first, zero, q)], axis=0)

        s = _dot_nt(qq, km)
        m = jnp.max(s, axis=1, keepdims=True)
        p = jnp.exp(s - m)
        l = jnp.sum(p, axis=1, keepdims=True)
        acc = _dot(p.astype(BF16), vm)

        def k_block(kb, c):
            k0 = pl.multiple_of(kb * tq, tq)
            s = _dot_nt(qq, k_ref[pl.ds(k0, tq), :])
            return _softmax_update(s, v_ref[pl.ds(k0, tq), :], *c)

        m, l, acc = lax.fori_loop(0, qi, k_block, (m, l, acc))

        s = _dot_nt(qq, k_ref[pl.ds(q0, tq), :])
        s = jnp.where(tri, s, NEG)
        m, l, acc = _softmax_update(s, v_ref[pl.ds(q0, tq), :], m, l, acc)

        o = acc / l
        od = o[:tq] - lam * o[tq:]
        o_ref[pl.ds(q0, tq), :] = _sub_ln(od, g).astype(o_ref.dtype)
        return carry

    lax.fori_loop(0, nq, q_tile, 0)


def _prompt_attention(l4, q, kb, vb, km, vm, g, tq):
    b, s, _ = q.shape
    seq = lambda bi, h: (bi, 0, h)
    const = lambda bi, h: (0, 0)
    return pl.pallas_call(
        functools.partial(_pattn_kernel, tq=tq),
        grid=(b, N_HEADS),
        in_specs=[
            pl.BlockSpec(l4.shape, const),
            pl.BlockSpec((None, s, LANES), seq),
            pl.BlockSpec((None, s, LANES), seq),
            pl.BlockSpec((None, s, LANES), seq),
            pl.BlockSpec((N_META, LANES), lambda bi, h: (0, h)),
            pl.BlockSpec((N_META, LANES), lambda bi, h: (0, h)),
            pl.BlockSpec((1, LANES), const),
        ],
        out_specs=pl.BlockSpec((None, s, LANES), seq),
        out_shape=jax.ShapeDtypeStruct((b, s, ATTN_DIM), BF16),
        compiler_params=_params("parallel", "parallel"),
        name="prompt_attn",
    )(l4, q, kb, vb, km, vm, g)


def _sattn_kernel(pt_ref, l4_ref, q_ref, kn_ref, vn_ref, g_ref, *rest):
    del pt_ref
    np_ = PAGES_PER_STEP
    k_refs = rest[:np_]
    v_refs = rest[np_:2 * np_]
    o_ref = rest[2 * np_]
    qq_ref, m_ref, l_ref, acc_ref = rest[2 * np_ + 1:]
    j = pl.program_id(1)
    rows = 16
    tpad = 8

    @pl.when(j == 0)
    def _():
        q = q_ref[...]
        first = lax.broadcasted_iota(jnp.int32, (tpad, LANES), 1) < HEAD_DIM
        kn = kn_ref[...]
        vn = vn_ref[...]
        qrow = lax.broadcasted_iota(jnp.int32, (rows, tpad), 0) % tpad
        kcol = lax.broadcasted_iota(jnp.int32, (rows, tpad), 1)
        ok = kcol <= qrow
        for h in range(N_HEADS):
            qh = q[:, h * LANES:(h + 1) * LANES]
            qq = jnp.concatenate([jnp.where(first, qh, 0.0), jnp.where(first, 0.0, qh)], axis=0)
            qq_ref[h * rows:(h + 1) * rows, :] = qq.astype(BF16)
            s = _dot_nt(qq.astype(BF16), kn[:, h * LANES:(h + 1) * LANES].astype(BF16))
            s = jnp.where(ok, s, NEG)
            m = jnp.max(s, axis=1, keepdims=True)
            p = jnp.exp(s - m)
            m_ref[h * rows:(h + 1) * rows, :] = m
            l_ref[h * rows:(h + 1) * rows, :] = jnp.sum(p, axis=1, keepdims=True)
            acc_ref[h * rows:(h + 1) * rows, :] = _dot(
                p.astype(BF16), vn[:, h * LANES:(h + 1) * LANES].astype(BF16))

    s_heads = []
    for h in range(N_HEADS):
        kh = jnp.concatenate(
            [r[pl.ds(h, PAGE_SIZE, stride=N_HEADS), :] for r in k_refs], axis=0).astype(BF16)
        s_heads.append(_dot_nt(qq_ref[h * rows:(h + 1) * rows, :], kh))
    s = jnp.concatenate(s_heads, axis=0)
    m_old = m_ref[...]
    m_new = jnp.maximum(m_old, jnp.max(s, axis=1, keepdims=True))
    alpha = jnp.exp(m_old - m_new)
    p = jnp.exp(s - m_new)
    m_ref[...] = m_new
    l_ref[...] = alpha * l_ref[...] + jnp.sum(p, axis=1, keepdims=True)
    pb = p.astype(BF16)
    for h in range(N_HEADS):
        vh = jnp.concatenate(
            [r[pl.ds(h, PAGE_SIZE, stride=N_HEADS), :] for r in v_refs], axis=0).astype(BF16)
        sl = slice(h * rows, (h + 1) * rows)
        acc_ref[sl, :] = alpha[sl] * acc_ref[sl, :] + _dot(pb[sl], vh)

    @pl.when(j == pl.num_programs(1) - 1)
    def _():
        lam = _lambda(l4_ref[...])
        o = acc_ref[...] / l_ref[...]
        g = g_ref[...]
        for h in range(N_HEADS):
            od = o[h * rows:h * rows + tpad] - lam * o[h * rows + tpad:(h + 1) * rows]
            o_ref[:, h * LANES:(h + 1) * LANES] = _sub_ln(od, g)


def _sample_attention(page_table, l4, q8, kn8, vn8, g, ck, cv):
    db, n_pages = page_table.shape
    np_ = PAGES_PER_STEP
    steps = n_pages // np_
    tok = lambda b, j, pt: (b, 0, 0)
    const = lambda b, j, pt: (0, 0)

    def page(r):
        return pl.BlockSpec((None, PAGE_SIZE * N_HEADS, VAL_DIM),
                            lambda b, j, pt: (pt[b, j * np_ + r], 0, 0))

    grid_spec = pltpu.PrefetchScalarGridSpec(
        num_scalar_prefetch=1,
        grid=(db, steps),
        in_specs=[
            pl.BlockSpec(l4.shape, const),
            pl.BlockSpec((None, 8, ATTN_DIM), tok),
            pl.BlockSpec((None, 8, ATTN_DIM), tok),
            pl.BlockSpec((None, 8, ATTN_DIM), tok),
            pl.BlockSpec((1, LANES), const),
        ] + [page(r) for r in range(np_)] + [page(r) for r in range(np_)],
        out_specs=pl.BlockSpec((None, 8, ATTN_DIM), tok),
        scratch_shapes=[
            pltpu.VMEM((N_HEADS * 16, LANES), BF16),
            pltpu.VMEM((N_HEADS * 16, 1), F32),
            pltpu.VMEM((N_HEADS * 16, 1), F32),
            pltpu.VMEM((N_HEADS * 16, LANES), F32),
        ],
    )
    return pl.pallas_call(
        _sattn_kernel,
        grid_spec=grid_spec,
        out_shape=jax.ShapeDtypeStruct((db, 8, ATTN_DIM), F32),
        compiler_params=_params("parallel", "arbitrary"),
        name="sample_attn",
    )(page_table, l4, q8, kn8, vn8, g, *([ck] * np_), *([cv] * np_))


def _conv_kernel(*refs, tt, rc, has_halo):
    if has_halo:
        hist_ref, halo_ref, u_ref, w_ref, b_ref, lg_ref, lb_ref, c_ref, win_ref, acc_ref, chunk_ref = refs
    else:
        hist_ref, u_ref, w_ref, b_ref, lg_ref, lb_ref, c_ref, win_ref, acc_ref, chunk_ref = refs
    ti = pl.program_id(1)
    chans = u_ref.shape[1]

    @pl.when(ti == 0)
    def _():
        win_ref[0:HIST_ROWS, :] = hist_ref[...]

    if has_halo:
        @pl.when(ti > 0)
        def _():
            win_ref[0:HIST_ROWS, :] = halo_ref[...]

    win_ref[HIST_ROWS:HIST_ROWS + tt, :] = u_ref[...]

    def rows(r, carry):
        r0 = pl.multiple_of(r * rc, rc)
        chunk_ref[...] = win_ref[pl.ds(r0, rc + HIST_ROWS), :]
        for cb in range(chans // LANES):
            cs = slice(cb * LANES, (cb + 1) * LANES)
            acc = jnp.zeros((rc, LANES), F32)
            for tap in range(CONV_WIDTH):
                acc = acc + chunk_ref[HIST_PAD + tap:HIST_PAD + tap + rc, cs] * w_ref[tap:tap + 1, cs]
            acc_ref[pl.ds(r0, rc), cs] = acc + b_ref[:, cs]
        return carry

    lax.fori_loop(0, tt // rc, rows, 0)

    y = acc_ref[...]
    mu = jnp.mean(y, axis=-1, keepdims=True)
    yc = y - mu
    var = jnp.mean(yc * yc, axis=-1, keepdims=True)
    z = yc * lax.rsqrt(var + EPS) * lg_ref[...] + lb_ref[...]
    c_ref[...] = (z * _sigmoid(z)).astype(c_ref.dtype)


def _conv_branch(hist, u, w, b, lg, lb, tt, rc, out_dtype):
    nb, t, chans = u.shape
    nt = t // tt
    has_halo = nt > 1
    shared_hist = hist.shape[0] == 1
    const = lambda bi, ti: (0, 0)
    in_specs = [pl.BlockSpec((None, HIST_ROWS, chans),
                             (lambda bi, ti: (0, 0, 0)) if shared_hist else (lambda bi, ti: (bi, 0, 0)))]
    args = [hist]
    if has_halo:
        per = tt // HIST_ROWS
        in_specs.append(pl.BlockSpec((None, HIST_ROWS, chans),
                                     lambda bi, ti: (bi, jnp.maximum(ti * per - 1, 0), 0)))
        args.append(u)
    in_specs += [
        pl.BlockSpec((None, tt, chans), lambda bi, ti: (bi, ti, 0)),
        pl.BlockSpec(w.shape, const),
        pl.BlockSpec((1, chans), const),
        pl.BlockSpec((1, chans), const),
        pl.BlockSpec((1, chans), const),
    ]
    args += [u, w, b, lg, lb]
    return pl.pallas_call(
        functools.partial(_conv_kernel, tt=tt, rc=rc, has_halo=has_halo),
        grid=(nb, nt),
        in_specs=in_specs,
        out_specs=pl.BlockSpec((None, tt, chans), lambda bi, ti: (bi, ti, 0)),
        out_shape=jax.ShapeDtypeStruct((nb, t, chans), out_dtype),
        scratch_shapes=[pltpu.VMEM((HIST_ROWS + tt, chans), F32), pltpu.VMEM((tt, chans), F32),
                        pltpu.VMEM((HIST_ROWS + rc, chans), F32)],
        compiler_params=_params("parallel", "arbitrary"),
        name="conv_branch",
    )(*args)


def _merge_kernel(o_ref, c_ref, ga_ref, gc_ref, x_ref, wa_ref, wc_ref, wo_ref, out_ref):
    ya = _dot(o_ref[...], wa_ref[...])
    yc = _dot(c_ref[...], wc_ref[...])
    h = ga_ref[...].astype(F32) * ya + gc_ref[...].astype(F32) * yc
    out_ref[...] = x_ref[...] + _dot(h.astype(BF16), wo_ref[...])


def _merge(o, c, ga, gc, x, wa, wc, wo, tm):
    m, d = x.shape
    row = lambda i: (i, 0)
    const = lambda i: (0, 0)
    resident = lambda shape: pl.BlockSpec(shape, const, pipeline_mode=pl.Buffered(1))
    return pl.pallas_call(
        _merge_kernel,
        grid=(m // tm,),
        in_specs=[
            pl.BlockSpec((tm, ATTN_DIM), row),
            pl.BlockSpec((tm, CONV_DIM), row),
            pl.BlockSpec((tm, d), row),
            pl.BlockSpec((tm, d), row),
            pl.BlockSpec((tm, d), row),
            resident(wa.shape), resident(wc.shape), resident(wo.shape),
        ],
        out_specs=pl.BlockSpec((tm, d), row),
        out_shape=jax.ShapeDtypeStruct((m, d), F32),
        compiler_params=_params("parallel"),
        name="merge",
    )(o, c, ga, gc, x, wa, wc, wo)


def _ffn_kernel(x_ref, g_ref, wg_ref, wu_ref, wd_ref, gf_ref, o_ref, xn_ref):
    f = pl.program_id(1)

    @pl.when(f == 0)
    def _():
        x = x_ref[...]
        ms = jnp.mean(x * x, axis=-1, keepdims=True)
        xn_ref[...] = (x * lax.rsqrt(ms + EPS) * g_ref[...]).astype(BF16)
        o_ref[...] = x

    xn = xn_ref[...]
    gate = _dot(xn, wg_ref[...])
    up = _dot(xn, wu_ref[...])
    hid = (gate * _sigmoid(gate) * up).astype(BF16)
    o_ref[...] += _dot(hid, wd_ref[...])

    @pl.when(f == pl.num_programs(1) - 1)
    def _():
        y = o_ref[...]
        ms = jnp.mean(y * y, axis=-1, keepdims=True)
        o_ref[...] = y * lax.rsqrt(ms + EPS) * gf_ref[...]


def _ffn(x, g, wg, wu, wd, gf, tm, tf):
    m, d = x.shape
    dff = wg.shape[1]
    row = lambda i, f: (i, 0)
    const = lambda i, f: (0, 0)
    return pl.pallas_call(
        _ffn_kernel,
        grid=(m // tm, dff // tf),
        in_specs=[
            pl.BlockSpec((tm, d), row),
            pl.BlockSpec((1, d), const),
            pl.BlockSpec((d, tf), lambda i, f: (0, f)),
            pl.BlockSpec((d, tf), lambda i, f: (0, f)),
            pl.BlockSpec((tf, d), lambda i, f: (f, 0)),
            pl.BlockSpec((1, d), const),
        ],
        out_specs=pl.BlockSpec((tm, d), row),
        out_shape=jax.ShapeDtypeStruct((m, d), F32),
        scratch_shapes=[pltpu.VMEM((tm, d), BF16)],
        compiler_params=_params("parallel", "arbitrary"),
        name="ffn",
    )(x, g, wg, wu, wd, gf)


def _rope_tables(pos):
    half = HEAD_DIM // 2
    inv = ROPE_THETA ** (-jnp.arange(half, dtype=F32) / half)
    ang = pos.astype(F32)[:, None] * inv[None, :]
    cos = jnp.cos(ang)
    sin = jnp.sin(ang)
    return jnp.tile(cos, (1, 4)), jnp.concatenate([-sin, sin, -sin, sin], axis=1)


def kernel(x_prompt, x_sample, cache_k, cache_v, state_conv, page_table, meta_tokens, norm_mix_g, w_in, lambda_q1, lambda_k1, lambda_q2, lambda_k2, subln_g, w_attn_proj, conv_dw_w, conv_dw_b, conv_ln_g, conv_ln_b, w_conv_proj, w_out, norm_ffn_g, w_ffn_gate, w_ffn_up, w_ffn_down, norm_final_g):
    b, seq, d = x_prompt.shape
    db, t = x_sample.shape[0], x_sample.shape[1]
    n_pages = page_table.shape[1]
    past_len = n_pages * PAGE_SIZE
    assert cache_k.shape[0] == 1 and t <= 8 and seq % 512 == 0 and n_pages % PAGES_PER_STEP == 0

    w_in_b = w_in[0].astype(BF16)
    wa_b = w_attn_proj[0].astype(BF16)
    wc_b = w_conv_proj[0].astype(BF16)
    wo_b = w_out[0].astype(BF16)
    wg_b = w_ffn_gate[0].astype(BF16)
    wu_b = w_ffn_up[0].astype(BF16)
    wd_b = w_ffn_down[0].astype(BF16)
    g_mix = norm_mix_g[0][None, :]
    g_ffn = norm_ffn_g[0][None, :]
    g_fin = norm_final_g[None, :]
    g_sub = subln_g[0][None, :]
    l4 = jnp.stack([lambda_q1[0], lambda_k1[0], lambda_q2[0], lambda_k2[0]])
    cw, cb = conv_dw_w[0], conv_dw_b[0][None, :]
    clg, clb = conv_ln_g[0][None, :], conv_ln_b[0][None, :]

    tm = 512
    cos_p, sin_p = _rope_tables(N_META + jnp.arange(seq, dtype=jnp.int32))
    xp = x_prompt.reshape(b * seq, d)
    q_p, k_p, kb_p, v_p, vb_p, u_p, ga_p, gc_p = _inproj(xp, g_mix, w_in_b, cos_p, sin_p, tm)

    n_s = db * t
    pos_small = jnp.concatenate([
        jnp.tile(past_len + jnp.arange(t, dtype=jnp.int32), db),
        jnp.arange(N_META, dtype=jnp.int32)])
    cos_s, sin_s = _rope_tables(pos_small)
    x_small = jnp.concatenate([x_sample.reshape(n_s, d), meta_tokens.astype(x_prompt.dtype)], axis=0)
    q_s, k_s, kb_s, v_s, vb_s, u_s, ga_s, gc_s = _inproj(
        x_small, g_mix, w_in_b, cos_s, sin_s, n_s + N_META)
    k_m, kb_m, v_m, vb_m, u_m = k_s[n_s:], kb_s[n_s:], v_s[n_s:], vb_s[n_s:], u_s[n_s:]

    o_p = _prompt_attention(
        l4, q_p.reshape(b, seq, ATTN_DIM), kb_p.reshape(b, seq, ATTN_DIM),
        vb_p.reshape(b, seq, ATTN_DIM), kb_m, vb_m, g_sub, tq=256).reshape(b * seq, ATTN_DIM)

    pad8 = lambda a: jnp.pad(a.reshape(db, t, ATTN_DIM), ((0, 0), (0, 8 - t), (0, 0)))
    ck = cache_k[0].reshape(cache_k.shape[1], PAGE_SIZE * N_HEADS, VAL_DIM)
    cv = cache_v[0].reshape(cache_v.shape[1], PAGE_SIZE * N_HEADS, VAL_DIM)
    o_s = _sample_attention(page_table, l4, pad8(q_s[:n_s].astype(F32)), pad8(k_s[:n_s]),
                            pad8(v_s[:n_s]), g_sub, ck, cv)
    o_s = o_s[:, :t].reshape(n_s, ATTN_DIM).astype(BF16)

    hist_p = jnp.concatenate([jnp.zeros((HIST_ROWS - N_META, CONV_DIM), F32), u_m], axis=0)[None]
    c_p = _conv_branch(hist_p, u_p.reshape(b, seq, CONV_DIM), cw, cb, clg, clb,
                       tt=512, rc=32, out_dtype=BF16).reshape(b * seq, CONV_DIM)
    hist_s = jnp.pad(state_conv[0].astype(F32), ((0, 0), (HIST_PAD, 0), (0, 0)))
    u_s8 = jnp.pad(u_s[:n_s].reshape(db, t, CONV_DIM), ((0, 0), (0, 8 - t), (0, 0)))
    c_s = _conv_branch(hist_s, u_s8, cw, cb, clg, clb, tt=8, rc=8, out_dtype=F32)
    c_s = c_s[:, :t].reshape(n_s, CONV_DIM).astype(BF16)

    x1_p = _merge(o_p, c_p, ga_p, gc_p, xp, wa_b, wc_b, wo_b, tm)
    y_p = _ffn(x1_p, g_ffn, wg_b, wu_b, wd_b, g_fin, tm, 512)
    xs = x_sample.reshape(n_s, d)
    x1_s = _merge(o_s, c_s, ga_s[:n_s], gc_s[:n_s], xs, wa_b, wc_b, wo_b, n_s)
    y_s = _ffn(x1_s, g_ffn, wg_b, wu_b, wd_b, g_fin, n_s, 512)

    heads = lambda a, n: a.reshape(1, n, -1, N_HEADS, VAL_DIM)
    with_meta = lambda m_rows, rows: jnp.concatenate(
        [jnp.broadcast_to(m_rows[None], (b, N_META, ATTN_DIM)), rows.reshape(b, seq, ATTN_DIM)], axis=1)
    k_prompt = heads(with_meta(k_m, k_p), b)
    v_prompt = heads(with_meta(v_m, v_p), b)
    keep = CONV_WIDTH - 1
    conv_prompt = u_p.reshape(b, seq, CONV_DIM)[:, seq - keep:][None]
    conv_sample = jnp.concatenate(
        [state_conv[0].astype(F32), u_s[:n_s].reshape(db, t, CONV_DIM)], axis=1)[:, -keep:][None]
    return (y_p.reshape(b, seq, d), y_s.reshape(db, t, d), k_prompt, v_prompt, conv_prompt,
            heads(k_s[:n_s], db), heads(v_s[:n_s], db), conv_sample)
```

```python
import functools
import math

import jax
import jax.numpy as jnp
from jax import lax
from jax.experimental import pallas as pl
from jax.experimental.pallas import tpu as pltpu

F32 = jnp.float32
BF16 = jnp.bfloat16

N_META = 16
N_HEADS = 8
HEAD_DIM = 64
VAL_DIM = 2 * HEAD_DIM
ATTN_DIM = N_HEADS * VAL_DIM
CONV_DIM = 1024
CONV_WIDTH = 31
PAGE_SIZE = 128
ROPE_THETA = 10000.0
EPS = 1e-6
NEG = -1e30
LAM_INIT = 0.8 - 0.6 * math.exp(-0.3 * 0)
Q_SCALE = HEAD_DIM ** -0.5 * math.log2(math.e)
MAX_FIXED_OFFSET_GAP = 100.0

LANES = 128
SUBLANES = 8
HIST_ROWS = 32
HIST_PAD = HIST_ROWS - (CONV_WIDTH - 1)
VMEM_LIMIT = 56 * 1024 * 1024
PAGES_PER_STEP = 8


def _dot(a, b):
    return jnp.dot(a, b, preferred_element_type=F32)


def _dot_nt(a, b):
    return lax.dot_general(a, b, (((1,), (1,)), ((), ())), preferred_element_type=F32)


def _sigmoid(x):
    return 1.0 / (1.0 + jnp.exp(-x))


def _params(*sem):
    return pltpu.CompilerParams(dimension_semantics=sem, vmem_limit_bytes=VMEM_LIMIT)


def _resident(shape):
    return pl.BlockSpec(shape, lambda i: (0, 0), pipeline_mode=pl.Buffered(1))


def _qk_kernel(x_ref, g_ref, w_ref, cos_ref, sin_ref, xn_ref, q_ref, k_ref, kb_ref):
    tm = x_ref.shape[0]
    x = x_ref[...]
    ms = jnp.mean(x * x, axis=-1, keepdims=True)
    xn = (x * lax.rsqrt(ms + EPS) * g_ref[...]).astype(BF16)
    xn_ref[...] = xn
    acc = _dot(xn, w_ref[...])

    lane = lax.broadcasted_iota(jnp.int32, (tm, LANES), 1)
    first = (lane % HEAD_DIM) < (HEAD_DIM // 2)
    cos = cos_ref[...]
    sin = sin_ref[...]
    for h in range(2 * N_HEADS):
        yh = acc[:, h * LANES:(h + 1) * LANES]
        rot = jnp.where(first, pltpu.roll(yh, LANES - HEAD_DIM // 2, 1),
                        pltpu.roll(yh, HEAD_DIM // 2, 1))
        r = yh * cos + rot * sin
        if h < N_HEADS:
            q_ref[:, h * LANES:(h + 1) * LANES] = (r * Q_SCALE).astype(BF16)
        else:
            cols = slice((h - N_HEADS) * LANES, (h - N_HEADS + 1) * LANES)
            k_ref[:, cols] = r
            kb_ref[:, cols] = r.astype(BF16)


def _vu_kernel(xn_ref, w_ref, v_ref, vb_ref, u_ref):
    acc = _dot(xn_ref[...], w_ref[...])
    v = acc[:, :ATTN_DIM]
    v_ref[...] = v
    vb_ref[...] = v.astype(BF16)
    u_ref[...] = acc[:, ATTN_DIM:ATTN_DIM + CONV_DIM] * _sigmoid(acc[:, ATTN_DIM + CONV_DIM:])


def _gates_kernel(xn_ref, w_ref, ga_ref, gc_ref):
    d = ga_ref.shape[1]
    acc = _dot(xn_ref[...], w_ref[...])
    ga_ref[...] = _sigmoid(acc[:, :d]).astype(BF16)
    gc_ref[...] = _sigmoid(acc[:, d:]).astype(BF16)


def _inproj(x, g, w_qk, w_vu, w_g, cos_t, sin_t, tm):
    m, d = x.shape
    n_pos = cos_t.shape[0] // tm
    row = lambda i: (i, 0)
    rows = lambda width: pl.BlockSpec((tm, width), row)
    sds = jax.ShapeDtypeStruct
    xn, q, k, kb = pl.pallas_call(
        _qk_kernel,
        grid=(m // tm,),
        in_specs=[rows(d), _resident((1, d)), _resident(w_qk.shape),
                  pl.BlockSpec((tm, LANES), lambda i: (i % n_pos, 0)),
                  pl.BlockSpec((tm, LANES), lambda i: (i % n_pos, 0))],
        out_specs=(rows(d), rows(ATTN_DIM), rows(ATTN_DIM), rows(ATTN_DIM)),
        out_shape=(sds((m, d), BF16), sds((m, ATTN_DIM), BF16), sds((m, ATTN_DIM), F32),
                   sds((m, ATTN_DIM), BF16)),
        compiler_params=_params("parallel"),
        name="inproj_qk",
    )(x, g, w_qk, cos_t, sin_t)
    v, vb, u = pl.pallas_call(
        _vu_kernel,
        grid=(m // tm,),
        in_specs=[rows(d), _resident(w_vu.shape)],
        out_specs=(rows(ATTN_DIM), rows(ATTN_DIM), rows(CONV_DIM)),
        out_shape=(sds((m, ATTN_DIM), F32), sds((m, ATTN_DIM), BF16), sds((m, CONV_DIM), F32)),
        compiler_params=_params("parallel"),
        name="inproj_vu",
    )(xn, w_vu)
    ga, gc = pl.pallas_call(
        _gates_kernel,
        grid=(m // tm,),
        in_specs=[rows(d), _resident(w_g.shape)],
        out_specs=(rows(d), rows(d)),
        out_shape=(sds((m, d), BF16), sds((m, d), BF16)),
        compiler_params=_params("parallel"),
        name="inproj_gates",
    )(xn, w_g)
    return q, k, kb, v, vb, u, ga, gc


def _lambda(l4):
    a = jnp.sum(l4[0:1] * l4[1:2], axis=1, keepdims=True)
    b = jnp.sum(l4[2:3] * l4[3:4], axis=1, keepdims=True)
    return jnp.exp(a) - jnp.exp(b) + LAM_INIT


def _sub_ln(od, g):
    ms = jnp.mean(od * od, axis=-1, keepdims=True)
    return od * lax.rsqrt(ms + EPS) * g * (1.0 - LAM_INIT)


def _softmax_update(s, v, m, l, acc):
    m_new = jnp.maximum(m, jnp.max(s, axis=1, keepdims=True))
    alpha = jnp.exp(m - m_new)
    p = jnp.exp(s - m_new)
    l = alpha * l + jnp.sum(p, axis=1, keepdims=True)
    acc = alpha * acc + _dot(p.astype(BF16), v)
    return m_new, l, acc


def _pattn_kernel(l4_ref, q_ref, k_ref, v_ref, km_ref, vm_ref, g_ref, o_ref,
                  q0t_ref, q1t_ref, vt_ref, vt2_ref, *, tq):
    s_len = q_ref.shape[0]
    nq = s_len // tq
    lam = _lambda(l4_ref[...])
    g_col = g_ref[...]
    comp0 = lax.broadcasted_iota(jnp.int32, (LANES, tq), 0) < HEAD_DIM
    tri = (lax.broadcasted_iota(jnp.int32, (tq, tq), 0)
           <= lax.broadcasted_iota(jnp.int32, (tq, tq), 1))

    for i in range(nq):
        rows = slice(i * tq, (i + 1) * tq)
        qt = q_ref[rows, :].astype(F32).T
        q0t_ref[i] = jnp.where(comp0, qt, 0.0).astype(BF16)
        q1t_ref[i] = jnp.where(comp0, 0.0, qt).astype(BF16)
        vt = v_ref[rows, :].astype(F32).T.astype(BF16)
        vt_ref[i] = vt
        vt2_ref[:, rows] = vt
    km = km_ref[...]
    vmt = vm_ref[...].astype(F32).T.astype(BF16)

    def finish(i, o0, o1):
        od = o0 - lam * o1
        ms = jnp.mean(od * od, axis=0, keepdims=True)
        on = od * lax.rsqrt(ms + EPS) * g_col * (1.0 - LAM_INIT)
        start = i * tq if isinstance(i, int) else pl.multiple_of(i * tq, tq)
        o_ref[pl.ds(start, tq), :] = on.T.astype(o_ref.dtype)

    kcol = jnp.maximum(jnp.max(jnp.abs(k_ref[...].astype(F32)), axis=0, keepdims=True),
                       jnp.max(jnp.abs(km.astype(F32)), axis=0, keepdims=True))
    lane0 = lax.broadcasted_iota(jnp.int32, (1, LANES), 1) < HEAD_DIM
    kmax = (jnp.max(jnp.where(lane0, kcol, 0.0)), jnp.max(jnp.where(lane0, 0.0, kcol)))
    gap = jnp.float32(-jnp.inf)
    for i in range(nq):
        for comp, qt_ref in enumerate((q0t_ref, q1t_ref)):
            qt = qt_ref[i]
            upper = jnp.sum(jnp.abs(qt.astype(F32)), axis=0, keepdims=True) * kmax[comp]
            c = jnp.max(_dot(km, qt), axis=0, keepdims=True)
            gap = jnp.maximum(gap, jnp.max(upper - c))

    def fixed_offset():
        for i in range(nq):
            outs = []
            for qt_ref in (q0t_ref, q1t_ref):
                qt = qt_ref[i]
                sm = _dot(km, qt)
                c = jnp.max(sm, axis=0, keepdims=True)
                p = jnp.exp2(sm - c)
                l = jnp.sum(p, axis=0, keepdims=True)
                acc = _dot(vmt, p.astype(BF16))
                if i > 0:
                    p = jnp.exp2(_dot(k_ref[0:i * tq, :], qt) - c)
                    l = l + jnp.sum(p, axis=0, keepdims=True)
                    acc = acc + _dot(vt2_ref[:, 0:i * tq], p.astype(BF16))
                rows = slice(i * tq, (i + 1) * tq)
                p = jnp.exp2(jnp.where(tri, _dot(k_ref[rows, :], qt), NEG) - c)
                l = l + jnp.sum(p, axis=0, keepdims=True)
                acc = acc + _dot(vt2_ref[:, rows], p.astype(BF16))
                outs.append(acc * (1.0 / l))
            finish(i, *outs)

    def update(st, vt, m, l, acc):
        m_new = jnp.maximum(m, jnp.max(st, axis=0, keepdims=True))
        alpha = jnp.exp2(m - m_new)
        p = jnp.exp2(st - m_new)
        l = alpha * l + jnp.sum(p, axis=0, keepdims=True)
        acc = alpha * acc + _dot(vt, p.astype(BF16))
        return m_new, l, acc

    def first(st, vt):
        m = jnp.max(st, axis=0, keepdims=True)
        p = jnp.exp2(st - m)
        return m, jnp.sum(p, axis=0, keepdims=True), _dot(vt, p.astype(BF16))

    def q_tile(qi, carry):
        q0 = pl.multiple_of(qi * tq, tq)
        q0t = q0t_ref[qi]
        q1t = q1t_ref[qi]
        c0 = first(_dot(km, q0t), vmt)
        c1 = first(_dot(km, q1t), vmt)

        def k_block(kb, c):
            k = k_ref[pl.ds(pl.multiple_of(kb * tq, tq), tq), :]
            vt = vt_ref[kb]
            return (update(_dot(k, q0t), vt, *c[0]), update(_dot(k, q1t), vt, *c[1]))

        c0, c1 = lax.fori_loop(0, qi, k_block, (c0, c1))

        k = k_ref[pl.ds(q0, tq), :]
        vt = vt_ref[qi]
        m0, l0, a0 = update(jnp.where(tri, _dot(k, q0t), NEG), vt, *c0)
        m1, l1, a1 = update(jnp.where(tri, _dot(k, q1t), NEG), vt, *c1)

        finish(qi, a0 / l0, a1 / l1)
        return carry

    def running_max():
        lax.fori_loop(0, nq, q_tile, 0)

    lax.cond(gap <= MAX_FIXED_OFFSET_GAP, fixed_offset, running_max)


def _prompt_attention(l4, q, kb, vb, km, vm, g, tq):
    b, s, _ = q.shape
    seq = lambda bi, h: (bi, 0, h)
    const = lambda bi, h: (0, 0)
    return pl.pallas_call(
        functools.partial(_pattn_kernel, tq=tq),
        grid=(b, N_HEADS),
        in_specs=[
            pl.BlockSpec(l4.shape, const),
            pl.BlockSpec((None, s, LANES), seq),
            pl.BlockSpec((None, s, LANES), seq),
            pl.BlockSpec((None, s, LANES), seq),
            pl.BlockSpec((N_META, LANES), lambda bi, h: (0, h)),
            pl.BlockSpec((N_META, LANES), lambda bi, h: (0, h)),
            pl.BlockSpec((LANES, 1), const),
        ],
        out_specs=pl.BlockSpec((None, s, LANES), seq),
        out_shape=jax.ShapeDtypeStruct((b, s, ATTN_DIM), BF16),
        scratch_shapes=[pltpu.VMEM((s // tq, LANES, tq), BF16)] * 3 + [pltpu.VMEM((LANES, s), BF16)],
        compiler_params=_params("parallel", "parallel"),
        name="prompt_attn",
    )(l4, q, kb, vb, km, vm, g)


QCOLS = 2 * SUBLANES


def _to_rows(vec):
    return jnp.broadcast_to(vec, (LANES, LANES)).T


def _sattn_kernel(pt_ref, l4_ref, qbd_ref, kn_ref, vn_ref, g_ref, *rest, t_new):
    del pt_ref
    np_ = PAGES_PER_STEP
    k_refs = rest[:np_]
    v_refs = rest[np_:2 * np_]
    o_ref = rest[2 * np_]
    kcat_ref, vcat_ref, m_ref, l_ref, acc_ref = rest[2 * np_ + 1:]
    j = pl.program_id(1)
    qbd = qbd_ref[...]

    def attend(kcat, vcat, valid):
        st = _dot(kcat, qbd)
        if valid is not None:
            st = jnp.where(valid, st, NEG)
        m_old = m_ref[...]
        m_new = jnp.maximum(m_old, jnp.max(st, axis=0, keepdims=True))
        alpha = jnp.exp2(m_old - m_new)
        p = jnp.exp2(st - m_new)
        m_ref[...] = m_new
        l_ref[...] = alpha * l_ref[...] + jnp.sum(p, axis=0, keepdims=True)
        pv = _dot(p.T.astype(BF16), vcat)
        alpha_rows = _to_rows(alpha)
        for h in range(N_HEADS):
            rows = slice(h * QCOLS, (h + 1) * QCOLS)
            acc_ref[rows, :] = alpha_rows[rows] * acc_ref[rows, :] + pv[rows, h * LANES:(h + 1) * LANES]

    @pl.when(j == 0)
    def _():
        m_ref[...] = jnp.full(m_ref.shape, NEG, F32)
        l_ref[...] = jnp.zeros(l_ref.shape, F32)
        acc_ref[...] = jnp.zeros(acc_ref.shape, F32)
        shape = (kn_ref.shape[0], LANES)
        key = lax.broadcasted_iota(jnp.int32, shape, 0)
        slot = lax.broadcasted_iota(jnp.int32, shape, 1) % SUBLANES
        attend(kn_ref[...].astype(BF16), vn_ref[...].astype(BF16), (key <= slot) & (key < t_new))

    for r in range(np_):
        for h in range(N_HEADS):
            dst = (slice(r * PAGE_SIZE, (r + 1) * PAGE_SIZE), slice(h * LANES, (h + 1) * LANES))
            kcat_ref[dst] = k_refs[r][pl.ds(h, PAGE_SIZE, stride=N_HEADS), :].astype(BF16)
            vcat_ref[dst] = v_refs[r][pl.ds(h, PAGE_SIZE, stride=N_HEADS), :].astype(BF16)
    attend(kcat_ref[...], vcat_ref[...], None)

    @pl.when(j == pl.num_programs(1) - 1)
    def _():
        lam = _lambda(l4_ref[...])
        o = acc_ref[...] / _to_rows(l_ref[...])
        g = g_ref[...]
        for h in range(N_HEADS):
            od = o[h * QCOLS:h * QCOLS + SUBLANES] - lam * o[h * QCOLS + SUBLANES:(h + 1) * QCOLS]
            o_ref[:, h * LANES:(h + 1) * LANES] = _sub_ln(od, g)


def _block_diag_queries(q, db, t):
    q5 = q.reshape(db, t, N_HEADS, 2, HEAD_DIM).astype(F32)
    q5 = jnp.pad(q5, ((0, 0), (0, SUBLANES - t), (0, 0), (0, 0), (0, 0)))
    qt = q5.transpose(0, 2, 3, 4, 1)
    eye = jnp.eye(2 * N_HEADS, dtype=F32).reshape(N_HEADS, 2, N_HEADS, 2)
    qbd = qt[:, :, :, :, None, None, :] * eye[None, :, :, None, :, :, None]
    return qbd.reshape(db, ATTN_DIM, N_HEADS * QCOLS).astype(BF16)


def _sample_attention(page_table, l4, qbd, kn, vn, g, ck, cv, t_new):
    db, n_pages = page_table.shape
    np_ = PAGES_PER_STEP
    steps = n_pages // np_
    tok = lambda b, j, pt: (b, 0, 0)
    const = lambda b, j, pt: (0, 0)

    def page(r):
        return pl.BlockSpec((None, PAGE_SIZE * N_HEADS, VAL_DIM),
                            lambda b, j, pt: (pt[b, j * np_ + r], 0, 0))

    grid_spec = pltpu.PrefetchScalarGridSpec(
        num_scalar_prefetch=1,
        grid=(db, steps),
        in_specs=[
            pl.BlockSpec(l4.shape, const),
            pl.BlockSpec((None,) + qbd.shape[1:], tok),
            pl.BlockSpec((None,) + kn.shape[1:], tok),
            pl.BlockSpec((None,) + vn.shape[1:], tok),
            pl.BlockSpec((1, LANES), const),
        ] + [page(r) for r in range(np_)] + [page(r) for r in range(np_)],
        out_specs=pl.BlockSpec((None, SUBLANES, ATTN_DIM), tok),
        scratch_shapes=[
            pltpu.VMEM((np_ * PAGE_SIZE, ATTN_DIM), BF16),
            pltpu.VMEM((np_ * PAGE_SIZE, ATTN_DIM), BF16),
            pltpu.VMEM((1, LANES), F32),
            pltpu.VMEM((1, LANES), F32),
            pltpu.VMEM((N_HEADS * QCOLS, LANES), F32),
        ],
    )
    return pl.pallas_call(
        functools.partial(_sattn_kernel, t_new=t_new),
        grid_spec=grid_spec,
        out_shape=jax.ShapeDtypeStruct((db, SUBLANES, ATTN_DIM), F32),
        compiler_params=_params("parallel", "arbitrary"),
        name="sample_attn",
    )(page_table, l4, qbd, kn, vn, g, *([ck] * np_), *([cv] * np_))


def _conv_kernel(*refs, tt, rc, has_halo):
    if has_halo:
        hist_ref, halo_ref, u_ref, w_ref, b_ref, lg_ref, lb_ref, c_ref, win_ref, acc_ref = refs
    else:
        hist_ref, u_ref, w_ref, b_ref, lg_ref, lb_ref, c_ref, win_ref, acc_ref = refs
    ti = pl.program_id(1)
    chans = u_ref.shape[1]

    @pl.when(ti == 0)
    def _():
        win_ref[0:HIST_ROWS, :] = hist_ref[...]

    if has_halo:
        @pl.when(ti > 0)
        def _():
            win_ref[0:HIST_ROWS, :] = halo_ref[...]

    win_ref[HIST_ROWS:HIST_ROWS + tt, :] = u_ref[...]

    def rows(r, carry):
        r0 = pl.multiple_of(r * rc, rc)
        for cb in range(chans // LANES):
            cs = slice(cb * LANES, (cb + 1) * LANES)
            blocks = [win_ref[pl.ds(r0 + HIST_ROWS - SUBLANES * (a + 1), rc + SUBLANES), cs]
                      for a in range(HIST_ROWS // SUBLANES)]
            out = None
            for s in range(SUBLANES):
                y = None
                for a, blk in enumerate(blocks):
                    d = SUBLANES * a + s
                    if d < CONV_WIDTH:
                        tap = CONV_WIDTH - 1 - d
                        term = blk * w_ref[tap:tap + 1, cs]
                        y = term if y is None else y + term
                z = y[SUBLANES - s:SUBLANES - s + rc]
                out = z if out is None else out + z
            acc_ref[pl.ds(r0, rc), cs] = out + b_ref[:, cs]
        return carry

    lax.fori_loop(0, tt // rc, rows, 0)

    y = acc_ref[...]
    mu = jnp.mean(y, axis=-1, keepdims=True)
    yc = y - mu
    var = jnp.mean(yc * yc, axis=-1, keepdims=True)
    z = yc * lax.rsqrt(var + EPS) * lg_ref[...] + lb_ref[...]
    c_ref[...] = (z * _sigmoid(z)).astype(c_ref.dtype)


def _conv_branch(hist, u, w, b, lg, lb, tt, rc, out_dtype):
    nb, t, chans = u.shape
    nt = t // tt
    has_halo = nt > 1
    shared_hist = hist.shape[0] == 1
    const = lambda bi, ti: (0, 0)
    in_specs = [pl.BlockSpec((None, HIST_ROWS, chans),
                             (lambda bi, ti: (0, 0, 0)) if shared_hist else (lambda bi, ti: (bi, 0, 0)))]
    args = [hist]
    if has_halo:
        per = tt // HIST_ROWS
        in_specs.append(pl.BlockSpec((None, HIST_ROWS, chans),
                                     lambda bi, ti: (bi, jnp.maximum(ti * per - 1, 0), 0)))
        args.append(u)
    in_specs += [
        pl.BlockSpec((None, tt, chans), lambda bi, ti: (bi, ti, 0)),
        pl.BlockSpec(w.shape, const),
        pl.BlockSpec((1, chans), const),
        pl.BlockSpec((1, chans), const),
        pl.BlockSpec((1, chans), const),
    ]
    args += [u, w, b, lg, lb]
    return pl.pallas_call(
        functools.partial(_conv_kernel, tt=tt, rc=rc, has_halo=has_halo),
        grid=(nb, nt),
        in_specs=in_specs,
        out_specs=pl.BlockSpec((None, tt, chans), lambda bi, ti: (bi, ti, 0)),
        out_shape=jax.ShapeDtypeStruct((nb, t, chans), out_dtype),
        scratch_shapes=[pltpu.VMEM((HIST_ROWS + tt, chans), F32), pltpu.VMEM((tt, chans), F32)],
        compiler_params=_params("parallel", "arbitrary"),
        name="conv_branch",
    )(*args)


def _merge_kernel(o_ref, c_ref, ga_ref, gc_ref, x_ref, wa_ref, wc_ref, wo_ref, out_ref):
    ya = _dot(o_ref[...], wa_ref[...])
    yc = _dot(c_ref[...], wc_ref[...])
    h = ga_ref[...].astype(F32) * ya + gc_ref[...].astype(F32) * yc
    out_ref[...] = x_ref[...] + _dot(h.astype(BF16), wo_ref[...])


def _merge(o, c, ga, gc, x, wa, wc, wo, tm):
    m, d = x.shape
    row = lambda i: (i, 0)
    return pl.pallas_call(
        _merge_kernel,
        grid=(m // tm,),
        in_specs=[
            pl.BlockSpec((tm, ATTN_DIM), row),
            pl.BlockSpec((tm, CONV_DIM), row),
            pl.BlockSpec((tm, d), row),
            pl.BlockSpec((tm, d), row),
            pl.BlockSpec((tm, d), row),
            _resident(wa.shape), _resident(wc.shape), _resident(wo.shape),
        ],
        out_specs=pl.BlockSpec((tm, d), row),
        out_shape=jax.ShapeDtypeStruct((m, d), F32),
        compiler_params=_params("parallel"),
        name="merge",
    )(o, c, ga, gc, x, wa, wc, wo)


def _ffn_kernel(x_ref, g_ref, wg_ref, wu_ref, wd_ref, gf_ref, o_ref, xn_ref):
    f = pl.program_id(1)

    @pl.when(f == 0)
    def _():
        x = x_ref[...]
        ms = jnp.mean(x * x, axis=-1, keepdims=True)
        xn_ref[...] = (x * lax.rsqrt(ms + EPS) * g_ref[...]).astype(BF16)
        o_ref[...] = x

    xn = xn_ref[...]
    gate = _dot(xn, wg_ref[...])
    up = _dot(xn, wu_ref[...])
    hid = (gate * _sigmoid(gate) * up).astype(BF16)
    o_ref[...] += _dot(hid, wd_ref[...])

    @pl.when(f == pl.num_programs(1) - 1)
    def _():
        y = o_ref[...]
        ms = jnp.mean(y * y, axis=-1, keepdims=True)
        o_ref[...] = y * lax.rsqrt(ms + EPS) * gf_ref[...]


def _ffn(x, g, wg, wu, wd, gf, tm, tf):
    m, d = x.shape
    dff = wg.shape[1]
    row = lambda i, f: (i, 0)
    const = lambda i, f: (0, 0)
    return pl.pallas_call(
        _ffn_kernel,
        grid=(m // tm, dff // tf),
        in_specs=[
            pl.BlockSpec((tm, d), row),
            pl.BlockSpec((1, d), const),
            pl.BlockSpec((d, tf), lambda i, f: (0, f)),
            pl.BlockSpec((d, tf), lambda i, f: (0, f)),
            pl.BlockSpec((tf, d), lambda i, f: (f, 0)),
            pl.BlockSpec((1, d), const),
        ],
        out_specs=pl.BlockSpec((tm, d), row),
        out_shape=jax.ShapeDtypeStruct((m, d), F32),
        scratch_shapes=[pltpu.VMEM((tm, d), BF16)],
        compiler_params=_params("parallel", "arbitrary"),
        name="ffn",
    )(x, g, wg, wu, wd, gf)


def _rope_tables(pos):
    half = HEAD_DIM // 2
    inv = ROPE_THETA ** (-jnp.arange(half, dtype=F32) / half)
    ang = pos.astype(F32)[:, None] * inv[None, :]
    cos = jnp.cos(ang)
    sin = jnp.sin(ang)
    return jnp.tile(cos, (1, 4)), jnp.concatenate([-sin, sin, -sin, sin], axis=1)


def kernel(x_prompt, x_sample, cache_k, cache_v, state_conv, page_table, meta_tokens, norm_mix_g, w_in, lambda_q1, lambda_k1, lambda_q2, lambda_k2, subln_g, w_attn_proj, conv_dw_w, conv_dw_b, conv_ln_g, conv_ln_b, w_conv_proj, w_out, norm_ffn_g, w_ffn_gate, w_ffn_up, w_ffn_down, norm_final_g):
    b, seq, d = x_prompt.shape
    db, t = x_sample.shape[0], x_sample.shape[1]
    n_pages = page_table.shape[1]
    past_len = n_pages * PAGE_SIZE
    assert cache_k.shape[0] == 1 and t <= 8 and seq % 512 == 0 and n_pages % PAGES_PER_STEP == 0

    qk_cols, vu_cols = 2 * ATTN_DIM, 3 * ATTN_DIM + 2 * CONV_DIM
    w_qk = w_in[0, :, :qk_cols].astype(BF16)
    w_vu = w_in[0, :, qk_cols:vu_cols].astype(BF16)
    w_g = w_in[0, :, vu_cols:].astype(BF16)
    wa_b = w_attn_proj[0].astype(BF16)
    wc_b = w_conv_proj[0].astype(BF16)
    wo_b = w_out[0].astype(BF16)
    wg_b = w_ffn_gate[0].astype(BF16)
    wu_b = w_ffn_up[0].astype(BF16)
    wd_b = w_ffn_down[0].astype(BF16)
    g_mix = norm_mix_g[0][None, :]
    g_ffn = norm_ffn_g[0][None, :]
    g_fin = norm_final_g[None, :]
    g_sub = subln_g[0][None, :]
    l4 = jnp.stack([lambda_q1[0], lambda_k1[0], lambda_q2[0], lambda_k2[0]])
    cw, cb = conv_dw_w[0], conv_dw_b[0][None, :]
    clg, clb = conv_ln_g[0][None, :], conv_ln_b[0][None, :]

    tm = 512
    cos_p, sin_p = _rope_tables(N_META + jnp.arange(seq, dtype=jnp.int32))
    xp = x_prompt.reshape(b * seq, d)
    q_p, k_p, kb_p, v_p, vb_p, u_p, ga_p, gc_p = _inproj(xp, g_mix, w_qk, w_vu, w_g, cos_p, sin_p, tm)

    n_s = db * t
    pos_small = jnp.concatenate([
        jnp.tile(past_len + jnp.arange(t, dtype=jnp.int32), db),
        jnp.arange(N_META, dtype=jnp.int32)])
    cos_s, sin_s = _rope_tables(pos_small)
    x_small = jnp.concatenate([x_sample.reshape(n_s, d), meta_tokens.astype(x_prompt.dtype)], axis=0)
    q_s, k_s, kb_s, v_s, vb_s, u_s, ga_s, gc_s = _inproj(
        x_small, g_mix, w_qk, w_vu, w_g, cos_s, sin_s, n_s + N_META)
    k_m, kb_m, v_m, vb_m, u_m = k_s[n_s:], kb_s[n_s:], v_s[n_s:], vb_s[n_s:], u_s[n_s:]

    o_p = _prompt_attention(
        l4, q_p.reshape(b, seq, ATTN_DIM), kb_p.reshape(b, seq, ATTN_DIM),
        vb_p.reshape(b, seq, ATTN_DIM), kb_m, vb_m, g_sub.reshape(VAL_DIM, 1),
        tq=256).reshape(b * seq, ATTN_DIM)

    pad16 = lambda a: jnp.pad(a.reshape(db, t, ATTN_DIM), ((0, 0), (0, 2 * SUBLANES - t), (0, 0)))
    ck = cache_k[0].reshape(cache_k.shape[1], PAGE_SIZE * N_HEADS, VAL_DIM)
    cv = cache_v[0].reshape(cache_v.shape[1], PAGE_SIZE * N_HEADS, VAL_DIM)
    o_s = _sample_attention(page_table, l4, _block_diag_queries(q_s[:n_s], db, t),
                            pad16(k_s[:n_s]), pad16(v_s[:n_s]), g_sub, ck, cv, t)
    o_s = o_s[:, :t].reshape(n_s, ATTN_DIM).astype(BF16)

    hist_p = jnp.concatenate([jnp.zeros((HIST_ROWS - N_META, CONV_DIM), F32), u_m], axis=0)[None]
    c_p = _conv_branch(hist_p, u_p.reshape(b, seq, CONV_DIM), cw, cb, clg, clb,
                       tt=512, rc=32, out_dtype=BF16).reshape(b * seq, CONV_DIM)
    hist_s = jnp.pad(state_conv[0].astype(F32), ((0, 0), (HIST_PAD, 0), (0, 0)))
    u_s8 = jnp.pad(u_s[:n_s].reshape(db, t, CONV_DIM), ((0, 0), (0, 8 - t), (0, 0)))
    c_s = _conv_branch(hist_s, u_s8, cw, cb, clg, clb, tt=8, rc=8, out_dtype=F32)
    c_s = c_s[:, :t].reshape(n_s, CONV_DIM).astype(BF16)

    x1_p = _merge(o_p, c_p, ga_p, gc_p, xp, wa_b, wc_b, wo_b, tm)
    y_p = _ffn(x1_p, g_ffn, wg_b, wu_b, wd_b, g_fin, tm, 512)
    xs = x_sample.reshape(n_s, d)
    x1_s = _merge(o_s, c_s, ga_s[:n_s], gc_s[:n_s], xs, wa_b, wc_b, wo_b, n_s)
    y_s = _ffn(x1_s, g_ffn, wg_b, wu_b, wd_b, g_fin, n_s, 512)

    heads = lambda a, n: a.reshape(1, n, -1, N_HEADS, VAL_DIM)
    with_meta = lambda m_rows, rows: jnp.concatenate(
        [jnp.broadcast_to(m_rows[None], (b, N_META, ATTN_DIM)), rows.reshape(b, seq, ATTN_DIM)], axis=1)
    k_prompt = heads(with_meta(k_m, k_p), b)
    v_prompt = heads(with_meta(v_m, v_p), b)
    keep = CONV_WIDTH - 1
    conv_prompt = u_p.reshape(b, seq, CONV_DIM)[:, seq - keep:][None]
    conv_sample = jnp.concatenate(
        [state_conv[0].astype(F32), u_s[:n_s].reshape(db, t, CONV_DIM)], axis=1)[:, -keep:][None]
    return (y_p.reshape(b, seq, d), y_s.reshape(db, t, d), k_prompt, v_prompt, conv_prompt,
            heads(k_s[:n_s], db), heads(v_s[:n_s], db), conv_sample)
```
